```python
import math
import jax
import jax.numpy as jnp
from jax import lax
import numpy as np

D_MODEL = 1024
BATCH = 16
SEQ = 2048
DEPTH = 2

F32 = jnp.float32
HEAD_DIM = 64
D_MIX = D_MODEL
GROUP_W = D_MIX // 4
POOL_WINDOWS = (2, 4, 8, 16)
POOL_GROUPS = 4
POOL_CH = GROUP_W // POOL_GROUPS
B_HEADS = GROUP_W // HEAD_DIM
B_KV_HEADS = 2
GRID_W = 64
ROPE_THETA = 10000.0
C_HEADS = GROUP_W // HEAD_DIM
C_QK_DIM = HEAD_DIM // 2
D_HEADS = GROUP_W // HEAD_DIM
D_KV_HEADS = 2
WINDOW = 128
Q_BLOCK = 128
D_FF = 2816
N_EXPERTS = 8
TOP_K = 2
D_FF_EXPERT = 3584
LN_EPS = 1e-5
RMS_EPS = 1e-6
DEEPNORM_ALPHA = (2 * DEPTH) ** 0.25
DEEPNORM_BETA = (8 * DEPTH) ** -0.25
N_DENSE = (DEPTH + 1) // 2
N_MOE = DEPTH // 2
COL_SIZES = (GROUP_W,
             B_HEADS * HEAD_DIM, B_KV_HEADS * HEAD_DIM, B_KV_HEADS * HEAD_DIM,
             C_HEADS * 2 * C_QK_DIM, C_HEADS * 2 * C_QK_DIM, C_HEADS * HEAD_DIM,
             D_HEADS * HEAD_DIM, D_KV_HEADS * HEAD_DIM, D_KV_HEADS * HEAD_DIM)
D_IN = sum(COL_SIZES)

kernel_name = 'hybrid_parallel_mixer_encoder'


def layer_norm(x, g, b):
    xf = x.astype(F32)
    mu = jnp.mean(xf, -1, keepdims=True)
    var = jnp.mean(jnp.square(xf - mu), -1, keepdims=True)
    return ((xf - mu) * lax.rsqrt(var + LN_EPS) * g.astype(F32) + b.astype(F32)).astype(x.dtype)


def rms_norm(x, g):
    xf = x.astype(F32)
    return (xf * lax.rsqrt(jnp.mean(xf * xf, -1, keepdims=True) + RMS_EPS) * g.astype(F32)).astype(x.dtype)


def split_columns(proj):
    parts, start = [], 0
    for w in COL_SIZES:
        parts.append(proj[..., start:start + w])
        start += w
    return parts


def alibi_slopes():
    n = C_HEADS + D_HEADS
    return 2.0 ** (-8.0 * jnp.arange(1, n + 1, dtype=F32) / n)


def axial_rope_tables(s):
    rows = s // GRID_W
    row = jnp.repeat(jnp.arange(rows, dtype=F32), GRID_W)
    col = (jnp.arange(s) % GRID_W).astype(F32)
    axis_dim = HEAD_DIM // 2
    inv_freq = ROPE_THETA ** (-jnp.arange(0, axis_dim, 2, dtype=F32) / axis_dim)
    ang = jnp.stack([row[:, None] * inv_freq, col[:, None] * inv_freq], axis=1)
    return jnp.cos(ang), jnp.sin(ang)


def apply_axial_rope(x, cos, sin):
    b, s, h, dh = x.shape
    xf = x.astype(F32).reshape(b, s, h, 2, dh // 2)
    x1, x2 = xf[..., :dh // 4], xf[..., dh // 4:]
    c, sn = cos[None, :, None], sin[None, :, None]
    out = jnp.concatenate([x1 * c - x2 * sn, x2 * c + x1 * sn], axis=-1)
    return out.reshape(b, s, h, dh).astype(x.dtype)


def multiscale_pool(u, w_pool, scale):
    b, s, _ = u.shape
    ug = u.reshape(b, s, POOL_GROUPS, POOL_CH)
    ugf = ug.astype(F32)
    cs = jnp.pad(jnp.cumsum(ugf, axis=1), ((0, 0), (1, 0), (0, 0), (0, 0)))
    t = jnp.arange(s)
    pooled = []
    for gi, w in enumerate(POOL_WINDOWS):
        lo = jnp.clip(t - w // 2, 0, s)
        hi = jnp.clip(t - w // 2 + w, 0, s)
        cs_g = cs[:, :, gi]
        cnt = (hi - lo).astype(F32)[None, :, None]
        pooled.append((cs_g[:, hi] - cs_g[:, lo]) / cnt)
    d = jnp.stack(pooled, axis=2) - ugf
    y = jnp.einsum('bsgc,gcd->bsgd', d.astype(u.dtype), w_pool)
    return y.reshape(b, s, GROUP_W) * scale


def global_gqa(q, k, v):
    b, s, hq, dh = q.shape
    hkv = k.shape[2]
    g = hq // hkv
    nb = s // Q_BLOCK
    qb = jnp.moveaxis(q.reshape(b, nb, Q_BLOCK, hkv, g, dh), 1, 0)
    scale = dh ** -0.5

    def one_block(qi):
        sc = jnp.einsum('bqkgd,bskd->bkgqs', qi, k, preferred_element_type=F32) * scale
        p = jax.nn.softmax(sc, axis=-1)
        return jnp.einsum('bkgqs,bskd->bqkgd', p.astype(v.dtype), v)

    o = lax.map(one_block, qb)
    return jnp.moveaxis(o, 0, 1).reshape(b, s, hq * dh)


def diff_attention(q, k, v, lam, lam_init, slopes, subln_w):
    b, s, h, _, dq = q.shape
    nb = s // Q_BLOCK
    qb = jnp.moveaxis(q.reshape(b, nb, Q_BLOCK, h, 2, dq), 1, 0)
    pos = jnp.arange(s)
    qpos = pos.reshape(nb, Q_BLOCK)
    scale = dq ** -0.5

    def one_block(args):
        qi, qp = args
        sc = jnp.einsum('bqhcd,bshcd->bhcqs', qi, k, preferred_element_type=F32) * scale
        dist = jnp.abs(qp[:, None] - pos[None, :]).astype(F32)
        sc = sc - slopes[None, :, None, None, None] * dist[None, None, None]
        p = jax.nn.softmax(sc, axis=-1)
        attn = p[:, :, 0] - lam * p[:, :, 1]
        return jnp.einsum('bhqs,bshd->bqhd', attn.astype(v.dtype), v)

    o = lax.map(one_block, (qb, qpos))
    o = jnp.moveaxis(o, 0, 1).reshape(b, s, h, v.shape[-1])
    o = rms_norm(o, subln_w) * (1.0 - lam_init)
    return o.reshape(b, s, h * v.shape[-1])


def windowed_gqa_sink(q, k, v, slopes, sink):
    b, s, hq, dh = q.shape
    hkv = k.shape[2]
    g = hq // hkv
    nb = s // Q_BLOCK
    pad = ((0, 0), (Q_BLOCK, Q_BLOCK), (0, 0), (0, 0))
    kp = jnp.pad(k, pad).reshape(b, nb + 2, Q_BLOCK, hkv, dh)
    vp = jnp.pad(v, pad).reshape(b, nb + 2, Q_BLOCK, hkv, dh)
    kband = jnp.concatenate([kp[:, :-2], kp[:, 1:-1], kp[:, 2:]], axis=2)
    vband = jnp.concatenate([vp[:, :-2], vp[:, 1:-1], vp[:, 2:]], axis=2)
    qb = q.reshape(b, nb, Q_BLOCK, hkv, g, dh)
    sc = jnp.einsum('bnqkgd,bnskd->bnkgqs', qb, kband, preferred_element_type=F32) * (dh ** -0.5)
    qpos = jnp.arange(s).reshape(nb, Q_BLOCK)
    kpos = (jnp.arange(nb)[:, None] - 1) * Q_BLOCK + jnp.arange(3 * Q_BLOCK)[None]
    dist = jnp.abs(qpos[:, :, None] - kpos[:, None, :])
    valid = (dist <= WINDOW) & (kpos[:, None, :] >= 0) & (kpos[:, None, :] < s)
    sl = slopes.reshape(hkv, g)[None, None, :, :, None, None]
    sc = sc - sl * dist.astype(F32)[None, :, None, None]
    sc = jnp.where(valid[None, :, None, None], sc, -jnp.inf)
    snk = sink.astype(F32).reshape(hkv, g)[None, None, :, :, None, None]
    m = jnp.maximum(jnp.max(sc, axis=-1, keepdims=True), snk)
    p = jnp.exp(sc - m)
    p = p / (jnp.sum(p, axis=-1, keepdims=True) + jnp.exp(snk - m))
    o = jnp.einsum('bnkgqs,bnskd->bnqkgd', p.astype(v.dtype), vband)
    return o.reshape(b, s, hq * dh)


def swiglu(x, w_gate, w_up, w_down):
    return (jax.nn.silu(x @ w_gate) * (x @ w_up)) @ w_down


def moe_swiglu(x, w_router, e_gate, e_up, e_down):
    b, s, d = x.shape
    xt = x.reshape(b * s, d)
    logits = jnp.dot(xt, w_router, preferred_element_type=F32)
    top_v, top_i = lax.top_k(logits, TOP_K)
    gates = jax.nn.softmax(top_v, axis=-1)
    combine = jnp.sum(jax.nn.one_hot(top_i, N_EXPERTS, dtype=F32) * gates[..., None], axis=1)
    y = jnp.zeros((b * s, d), F32)
    for e in range(N_EXPERTS):
        h = jax.nn.silu(xt @ e_gate[e]) * (xt @ e_up[e])
        y = y + combine[:, e:e + 1] * (h @ e_down[e]).astype(F32)
    return y.astype(x.dtype).reshape(b, s, d)


def setup_inputs(seed: int = 0) -> dict:
    key = jax.random.key(seed)
    ks = jax.random.split(key, 32)
    L = DEPTH

    def nrm(k, shape, scale):
        return jax.random.normal(k, shape, F32) * scale

    return {
        'x': nrm(ks[0], (BATCH, SEQ, D_MODEL), 1.0),
        'ln_in_g': 1.0 + nrm(ks[1], (D_MODEL,), 0.05),
        'ln_in_b': nrm(ks[2], (D_MODEL,), 0.02),
        'w_in': nrm(ks[3], (L, D_MODEL, D_IN), D_MODEL ** -0.5),
        'w_pool': nrm(ks[4], (L, POOL_GROUPS, POOL_CH, POOL_CH), POOL_CH ** -0.5),
        'pool_scale': 1.0 + nrm(ks[5], (L, GROUP_W), 0.1),
        'qn_w': 1.0 + nrm(ks[6], (L, HEAD_DIM), 0.05),
        'kn_w': 1.0 + nrm(ks[7], (L, HEAD_DIM), 0.05),
        'lam_q1': nrm(ks[8], (L, C_QK_DIM), 0.1),
        'lam_k1': nrm(ks[9], (L, C_QK_DIM), 0.1),
        'lam_q2': nrm(ks[10], (L, C_QK_DIM), 0.1),
        'lam_k2': nrm(ks[11], (L, C_QK_DIM), 0.1),
        'subln_w': 1.0 + nrm(ks[12], (L, HEAD_DIM), 0.05),
        'sink': nrm(ks[13], (L, D_HEADS), 0.5),
        'w_out': nrm(ks[14], (L, D_MIX, D_MODEL), D_MIX ** -0.5 * DEEPNORM_BETA),
        'ln1_g': 1.0 + nrm(ks[15], (L, D_MODEL), 0.05),
        'ln1_b': nrm(ks[16], (L, D_MODEL), 0.02),
        'w_gate': nrm(ks[17], (N_DENSE, D_MODEL, D_FF), D_MODEL ** -0.5),
        'w_up': nrm(ks[18], (N_DENSE, D_MODEL, D_FF), D_MODEL ** -0.5),
        'w_down': nrm(ks[19], (N_DENSE, D_FF, D_MODEL), D_FF ** -0.5 * DEEPNORM_BETA),
        'w_router': nrm(ks[20], (N_MOE, D_MODEL, N_EXPERTS), D_MODEL ** -0.5),
        'e_gate': nrm(ks[21], (N_MOE, N_EXPERTS, D_MODEL, D_FF_EXPERT), D_MODEL ** -0.5),
        'e_up': nrm(ks[22], (N_MOE, N_EXPERTS, D_MODEL, D_FF_EXPERT), D_MODEL ** -0.5),
        'e_down': nrm(ks[23], (N_MOE, N_EXPERTS, D_FF_EXPERT, D_MODEL), D_FF_EXPERT ** -0.5 * DEEPNORM_BETA),
        'ln2_g': 1.0 + nrm(ks[24], (L, D_MODEL), 0.05),
        'ln2_b': nrm(ks[25], (L, D_MODEL), 0.02),
    }


def reference(x, ln_in_g, ln_in_b, w_in, w_pool, pool_scale, qn_w, kn_w, lam_q1, lam_k1,
              lam_q2, lam_k2, subln_w, sink, w_out, ln1_g, ln1_b, w_gate, w_up, w_down,
              w_router, e_gate, e_up, e_down, ln2_g, ln2_b):
    b, s, _ = x.shape
    cos, sin = axial_rope_tables(s)
    slopes = alibi_slopes()
    slopes_c, slopes_d = slopes[0::2], slopes[1::2]
    x = layer_norm(x, ln_in_g, ln_in_b)
    for l in range(DEPTH):
        proj = jnp.einsum('bsd,de->bse', x, w_in[l])
        (u_a, q_b, k_b, v_b, q_c, k_c, v_c, q_d, k_d, v_d) = split_columns(proj)
        y_a = multiscale_pool(u_a, w_pool[l], pool_scale[l])
        qb_ = apply_axial_rope(rms_norm(q_b.reshape(b, s, B_HEADS, HEAD_DIM), qn_w[l]), cos, sin)
        kb_ = apply_axial_rope(rms_norm(k_b.reshape(b, s, B_KV_HEADS, HEAD_DIM), kn_w[l]), cos, sin)
        y_b = global_gqa(qb_, kb_, v_b.reshape(b, s, B_KV_HEADS, HEAD_DIM))
        lam_init = 0.8 - 0.6 * math.exp(-0.3 * l)
        lam = (jnp.exp(jnp.sum((lam_q1[l] * lam_k1[l]).astype(F32)))
               - jnp.exp(jnp.sum((lam_q2[l] * lam_k2[l]).astype(F32))) + lam_init)
        y_c = diff_attention(q_c.reshape(b, s, C_HEADS, 2, C_QK_DIM),
                             k_c.reshape(b, s, C_HEADS, 2, C_QK_DIM),
                             v_c.reshape(b, s, C_HEADS, HEAD_DIM),
                             lam, lam_init, slopes_c, subln_w[l])
        y_d = windowed_gqa_sink(q_d.reshape(b, s, D_HEADS, HEAD_DIM),
                                k_d.reshape(b, s, D_KV_HEADS, HEAD_DIM),
                                v_d.reshape(b, s, D_KV_HEADS, HEAD_DIM),
                                slopes_d, sink[l])
        mix = jnp.concatenate([y_a, y_b, y_c, y_d], axis=-1) @ w_out[l]
        x = layer_norm(DEEPNORM_ALPHA * x + mix, ln1_g[l], ln1_b[l])
        i = l // 2
        if l % 2 == 0:
            f = swiglu(x, w_gate[i], w_up[i], w_down[i])
        else:
            f = moe_swiglu(x, w_router[i], e_gate[i], e_up[i], e_down[i])
        x = layer_norm(DEEPNORM_ALPHA * x + f, ln2_g[l], ln2_b[l])
    return x
```

```python
import functools
import math

import numpy as np
import jax
import jax.numpy as jnp
from jax import lax
from jax.experimental import pallas as pl
from jax.experimental.pallas import tpu as pltpu

F32 = jnp.float32
BF16 = jnp.bfloat16

D_MODEL = 1024
DEPTH = 2
HEAD_DIM = 64
GROUP_W = 256
POOL_WINDOWS = (2, 4, 8, 16)
GRID_W = 64
ROPE_THETA = 10000.0
C_QK_DIM = 32
WINDOW = 128
D_FF = 2816
N_EXPERTS = 8
D_FF_EXPERT = 3584
LN_EPS = 1e-5
RMS_EPS = 1e-6
DEEPNORM_ALPHA = (2 * DEPTH) ** 0.25
N_ALIBI = 8
LOG2E = math.log2(math.e)
NEG_BIG = -1e30

LANE = 128
PROJ_W = 10 * GROUP_W
VMEM_LIMIT = 56 * 1024 * 1024

TM_PROJ = 512
TQ_B = 256
TQ_C = 128
TQ_D = 256
TM_FFN = 512
TF_FFN = 1408
TM_MOE = 512
TF_MOE = 512
TM_GATHER = 512
TM_COMB = 256


def _cparams(sem):
    return pltpu.CompilerParams(dimension_semantics=sem, vmem_limit_bytes=VMEM_LIMIT)


def _layer_norm(y, g, b):
    mu = jnp.mean(y, axis=-1, keepdims=True)
    yc = y - mu
    var = jnp.mean(yc * yc, axis=-1, keepdims=True)
    return yc * lax.rsqrt(var + LN_EPS) * g + b


def _segment_mean(s, bd):
    hi = s.astype(BF16)
    lo = (s - hi.astype(F32)).astype(BF16)
    return jnp.dot(hi, bd, preferred_element_type=F32) + jnp.dot(lo, bd, preferred_element_type=F32)


def _norm_rope(x, bd, t1, t2):
    r = lax.rsqrt(_segment_mean(x * x, bd) + RMS_EPS)
    lane = lax.broadcasted_iota(jnp.int32, x.shape, 1)
    n = x.shape[1]
    partner = jnp.where((lane & 16) == 0, pltpu.roll(x, n - 16, 1), pltpu.roll(x, 16, 1))
    return r * (x * t1 + partner * t2)


def _inproj_body(apply_ln, sc_c, sc_d, *refs):
    if apply_ln:
        x_ref, g_ref, b_ref, w_ref, bd_ref, t1q, t2q, t1k, t2k, xn_ref, p_ref = refs
        xn = _layer_norm(x_ref[...], g_ref[...], b_ref[...])
        xn_ref[...] = xn
    else:
        x_ref, w_ref, bd_ref, t1q, t2q, t1k, t2k, p_ref = refs
        xn = x_ref[...]
    xb = xn.astype(BF16)
    for j in range(PROJ_W // GROUP_W):
        cols = slice(j * GROUP_W, (j + 1) * GROUP_W)
        p = jnp.dot(xb, w_ref[:, cols], preferred_element_type=F32)
        if j == 1:
            p = _norm_rope(p, bd_ref[...], t1q[...], t2q[...])
        elif j == 2:
            p = _norm_rope(p, bd_ref[...], t1k[...], t2k[...])
        elif j == 4:
            p = p * sc_c
        elif j == 7:
            p = p * sc_d
        p_ref[:, cols] = p.astype(BF16)


def _inproj(x, ln_g, ln_b, w, bd, t1q, t2q, t1k, t2k, apply_ln, seq):
    t = x.shape[0]
    tm = TM_PROJ
    nseq = seq // tm
    row = lambda i: (i, 0)
    fixed = lambda i: (0, 0)
    tab = lambda i: (i % nseq, 0)
    in_specs = [pl.BlockSpec((tm, D_MODEL), row)]
    args = [x]
    if apply_ln:
        in_specs += [pl.BlockSpec((1, D_MODEL), fixed), pl.BlockSpec((1, D_MODEL), fixed)]
        args += [ln_g, ln_b]
    in_specs += [pl.BlockSpec((D_MODEL, PROJ_W), fixed), pl.BlockSpec((GROUP_W, GROUP_W), fixed)]
    in_specs += [pl.BlockSpec((tm, GROUP_W), tab)] * 4
    args += [w, bd, t1q, t2q, t1k, t2k]
    out_shape = [jax.ShapeDtypeStruct((t, PROJ_W), BF16)]
    out_specs = [pl.BlockSpec((tm, PROJ_W), row)]
    if apply_ln:
        out_shape = [jax.ShapeDtypeStruct((t, D_MODEL), F32)] + out_shape
        out_specs = [pl.BlockSpec((tm, D_MODEL), row)] + out_specs
    sc_c = C_QK_DIM ** -0.5 * LOG2E
    sc_d = HEAD_DIM ** -0.5 * LOG2E
    return pl.pallas_call(
        functools.partial(_inproj_body, apply_ln, sc_c, sc_d),
        grid=(t // tm,), in_specs=in_specs, out_specs=out_specs, out_shape=out_shape,
        compiler_params=_cparams(("parallel",)), name="inproj_ln" if apply_ln else "inproj",
    )(*args)


POOL_PAD = 8
POOL_TAIL = 24


def _pool_body(u_ref, w_ref, sc_ref, o_ref, e_ref, r2_ref, r4_ref, r8_ref):
    s = u_ref.shape[1]
    z = jnp.dot(u_ref[0], w_ref[...], preferred_element_type=F32)
    e_ref[0:POOL_PAD, :] = jnp.zeros((POOL_PAD, GROUP_W), F32)
    e_ref[POOL_PAD + s:POOL_PAD + s + POOL_TAIL, :] = jnp.zeros((POOL_TAIL, GROUP_W), F32)
    e_ref[POOL_PAD:POOL_PAD + s, :] = z
    r2_ref[...] = e_ref[0:s + 24, :] + e_ref[1:s + 25, :]
    r4_ref[...] = r2_ref[0:s + 16, :] + r2_ref[2:s + 18, :]
    r8_ref[...] = r4_ref[0:s + 8, :] + r4_ref[4:s + 12, :]
    a16 = r8_ref[0:s, :] + r8_ref[8:s + 8, :]
    a8 = r8_ref[4:s + 4, :]
    a4 = r4_ref[6:s + 6, :]
    a2 = r2_ref[7:s + 7, :]
    t = lax.broadcasted_iota(jnp.int32, (s, 1), 0)

    def mean(a, w):
        lo = jnp.clip(t - w // 2, 0, s)
        hi = jnp.clip(t - w // 2 + w, 0, s)
        return a / (hi - lo).astype(F32)

    lane = lax.broadcasted_iota(jnp.int32, (s, GROUP_W), 1)
    pooled = jnp.where(lane < 64, mean(a2, 2),
                       jnp.where(lane < 128, mean(a4, 4),
                                 jnp.where(lane < 192, mean(a8, 8), mean(a16, 16))))
    o_ref[0] = ((pooled - z) * sc_ref[...]).astype(BF16)


def _pool(proj3, w_blk, scale):
    b, s, _ = proj3.shape
    return pl.pallas_call(
        _pool_body,
        grid=(b,),
        in_specs=[pl.BlockSpec((1, s, GROUP_W), lambda i: (i, 0, 0)),
                  pl.BlockSpec((GROUP_W, GROUP_W), lambda i: (0, 0)),
                  pl.BlockSpec((1, GROUP_W), lambda i: (0, 0))],
        out_specs=pl.BlockSpec((1, s, GROUP_W), lambda i: (i, 0, 0)),
        out_shape=jax.ShapeDtypeStruct((b, s, GROUP_W), BF16),
        scratch_shapes=[pltpu.VMEM((s + POOL_PAD + POOL_TAIL, GROUP_W), F32),
                        pltpu.VMEM((s + 24, GROUP_W), F32),
                        pltpu.VMEM((s + 16, GROUP_W), F32),
                        pltpu.VMEM((s + 8, GROUP_W), F32)],
        compiler_params=_cparams(("parallel",)), name="pool",
    )(proj3, w_blk, scale)


_NT = (((1,), (1,)), ((), ()))


def _stack_masked(q, bounds):
    lane = lax.broadcasted_iota(jnp.int32, q.shape, 1)
    zero = jnp.zeros_like(q)
    return jnp.concatenate([jnp.where((lane >= lo) & (lane < hi), q, zero) for lo, hi in bounds], axis=0)


def _halves(o0, o1):
    lane = lax.broadcasted_iota(jnp.int32, o0.shape, 1)
    return jnp.where(lane < HEAD_DIM, o0, o1)


def _attn_b_body(q_ref, k_ref, v_ref, o_ref):
    tq = q_ref.shape[1]
    qq = _stack_masked(q_ref[0], ((0, 64), (64, 128)))
    s = lax.dot_general(qq, k_ref[0], _NT, preferred_element_type=F32)
    m = jnp.max(s, axis=-1, keepdims=True)
    p = jnp.exp2(s - m)
    l = jnp.sum(p, axis=-1, keepdims=True)
    o = jnp.dot(p.astype(BF16), v_ref[0], preferred_element_type=F32) / l
    o_ref[0] = _halves(o[:tq], o[tq:]).astype(BF16)


def _attn_b(proj3):
    b, s, _ = proj3.shape
    tq = TQ_B
    return pl.pallas_call(
        _attn_b_body,
        grid=(b, 2, s // tq),
        in_specs=[pl.BlockSpec((1, tq, LANE), lambda i, g, t: (i, t, 2 + g)),
                  pl.BlockSpec((1, s, LANE), lambda i, g, t: (i, 0, 4 + g)),
                  pl.BlockSpec((1, s, LANE), lambda i, g, t: (i, 0, 6 + g))],
        out_specs=pl.BlockSpec((1, tq, LANE), lambda i, g, t: (i, t, g)),
        out_shape=jax.ShapeDtypeStruct((b, s, GROUP_W), BF16),
        compiler_params=_cparams(("parallel", "parallel", "parallel")), name="attn_global",
    )(proj3, proj3, proj3)


def _attn_c_body(lam_init, slopes, q_ref, k_ref, v_ref, lam_ref, w_ref, o_ref):
    tq = q_ref.shape[1]
    s_len = k_ref.shape[1]
    pair = pl.program_id(1)
    qi = pl.program_id(2)
    lv = lam_ref[...]
    lam = (jnp.exp(jnp.sum(lv[0:1] * lv[1:2], axis=-1, keepdims=True))
           - jnp.exp(jnp.sum(lv[2:3] * lv[3:4], axis=-1, keepdims=True)) + lam_init)
    qq = _stack_masked(q_ref[0], ((0, 32), (32, 64), (64, 96), (96, 128)))
    s = lax.dot_general(qq, k_ref[0], _NT, preferred_element_type=F32)
    pos_q = qi * tq + lax.broadcasted_iota(jnp.int32, (tq, 1), 0)
    pos_k = lax.broadcasted_iota(jnp.int32, (1, s_len), 1)
    dist = jnp.abs(pos_q - pos_k).astype(F32)
    v = v_ref[0]
    outs = []
    for j in range(2):
        slope = jnp.where(pair == 0, slopes[j], slopes[2 + j]) * LOG2E
        bias = slope * dist
        maps = []
        for c in range(2):
            sc = s[(2 * j + c) * tq:(2 * j + c + 1) * tq] - bias
            m = jnp.max(sc, axis=-1, keepdims=True)
            p = jnp.exp2(sc - m)
            maps.append((p, jnp.sum(p, axis=-1, keepdims=True)))
        attn = maps[0][0] * (1.0 / maps[0][1]) - maps[1][0] * (lam / maps[1][1])
        outs.append(jnp.dot(attn.astype(BF16), v, preferred_element_type=F32))
    o = _halves(outs[0], outs[1])
    lane = lax.broadcasted_iota(jnp.int32, o.shape, 1)
    sq = o * o
    ms0 = jnp.sum(jnp.where(lane < HEAD_DIM, sq, 0.0), axis=-1, keepdims=True)
    ms1 = jnp.sum(jnp.where(lane < HEAD_DIM, 0.0, sq), axis=-1, keepdims=True)
    ms = jnp.where(lane < HEAD_DIM, ms0, ms1) * (1.0 / HEAD_DIM)
    o_ref[0] = (o * lax.rsqrt(ms + RMS_EPS) * w_ref[...]).astype(BF16)


def _attn_c(proj3, lam_vecs, subln, lam_init, slopes):
    b, s, _ = proj3.shape
    tq = TQ_C
    return pl.pallas_call(
        functools.partial(_attn_c_body, lam_init, slopes),
        grid=(b, 2, s // tq),
        in_specs=[pl.BlockSpec((1, tq, LANE), lambda i, g, t: (i, t, 8 + g)),
                  pl.BlockSpec((1, s, LANE), lambda i, g, t: (i, 0, 10 + g)),
                  pl.BlockSpec((1, s, LANE), lambda i, g, t: (i, 0, 12 + g)),
                  pl.BlockSpec((4, C_QK_DIM), lambda i, g, t: (0, 0)),
                  pl.BlockSpec((1, LANE), lambda i, g, t: (0, 0))],
        out_specs=pl.BlockSpec((1, tq, LANE), lambda i, g, t: (i, t, g)),
        out_shape=jax.ShapeDtypeStruct((b, s, GROUP_W), BF16),
        compiler_params=_cparams(("parallel", "parallel", "parallel")), name="attn_diff",
    )(proj3, proj3, proj3, lam_vecs, subln)


def _attn_d_body(slopes, q_ref, k_ref, v_ref, sink_ref, o_ref):
    tq = q_ref.shape[1]
    s_len = k_ref.shape[1]
    kw = tq + 2 * WINDOW
    grp = pl.program_id(1)
    qi = pl.program_id(2)
    start = pl.multiple_of(jnp.clip(qi * tq - WINDOW, 0, s_len - kw), WINDOW)
    k = k_ref[0, pl.ds(start, kw), :]
    v = v_ref[0, pl.ds(start, kw), :]
    qq = _stack_masked(q_ref[0], ((0, 64), (64, 128)))
    s = lax.dot_general(qq, k, _NT, preferred_element_type=F32)
    pos_q = qi * tq + lax.broadcasted_iota(jnp.int32, (tq, 1), 0)
    pos_k = start + lax.broadcasted_iota(jnp.int32, (1, kw), 1)
    dist = jnp.abs(pos_q - pos_k)
    valid = dist <= WINDOW
    dist = dist.astype(F32)
    outs = []
    for j in range(2):
        slope = jnp.where(grp == 0, slopes[j], slopes[2 + j]) * LOG2E
        snk = sink_ref[2 * grp + j]
        sc = jnp.where(valid, s[j * tq:(j + 1) * tq] - slope * dist, NEG_BIG)
        m = jnp.maximum(jnp.max(sc, axis=-1, keepdims=True), snk)
        p = jnp.exp2(sc - m)
        l = jnp.sum(p, axis=-1, keepdims=True) + jnp.exp2(snk - m)
        outs.append(jnp.dot(p.astype(BF16), v, preferred_element_type=F32) / l)
    o_ref[0] = _halves(outs[0], outs[1]).astype(BF16)


def _attn_d(proj3, sink2, slopes):
    b, s, _ = proj3.shape
    tq = TQ_D
    return pl.pallas_call(
        functools.partial(_attn_d_body, slopes),
        grid=(b, 2, s // tq),
        in_specs=[pl.BlockSpec((1, tq, LANE), lambda i, g, t: (i, t, 14 + g)),
                  pl.BlockSpec((1, s, LANE), lambda i, g, t: (i, 0, 16 + g)),
                  pl.BlockSpec((1, s, LANE), lambda i, g, t: (i, 0, 18 + g)),
                  pl.BlockSpec(memory_space=pltpu.SMEM)],
        out_specs=pl.BlockSpec((1, tq, LANE), lambda i, g, t: (i, t, g)),
        out_shape=jax.ShapeDtypeStruct((b, s, GROUP_W), BF16),
        compiler_params=_cparams(("parallel", "parallel", "parallel")), name="attn_window",
    )(proj3, proj3, proj3, sink2)


def _top2(logits):
    lane = lax.broadcasted_iota(jnp.int32, logits.shape, 1).astype(F32)
    lg = jnp.where(lane < N_EXPERTS, logits, -jnp.inf)
    v1 = jnp.max(lg, axis=-1, keepdims=True)
    i1 = jnp.min(jnp.where(lg == v1, lane, float(LANE)), axis=-1, keepdims=True)
    lg2 = jnp.where(lane == i1, -jnp.inf, lg)
    v2 = jnp.max(lg2, axis=-1, keepdims=True)
    i2 = jnp.min(jnp.where(lg2 == v2, lane, float(LANE)), axis=-1, keepdims=True)
    e = jnp.exp(v2 - v1)
    g1 = 1.0 / (1.0 + e)
    return i1, i2, g1, e * g1


def _outproj_body(route, *refs):
    if route:
        ya, yb, yc, yd, w_ref, x_ref, g_ref, b_ref, wrh_ref, wrl_ref, o_ref, r_ref = refs
    else:
        ya, yb, yc, yd, w_ref, x_ref, g_ref, b_ref, o_ref = refs
    acc = None
    for j, y in enumerate((ya, yb, yc, yd)):
        part = jnp.dot(y[...], w_ref[j * GROUP_W:(j + 1) * GROUP_W, :], preferred_element_type=F32)
        acc = part if acc is None else acc + part
    x1 = _layer_norm(DEEPNORM_ALPHA * x_ref[...] + acc, g_ref[...], b_ref[...])
    o_ref[...] = x1
    if route:
        xh = x1.astype(BF16)
        xl = (x1 - xh.astype(F32)).astype(BF16)
        logits = (jnp.dot(xh, wrh_ref[...], preferred_element_type=F32)
                  + jnp.dot(xl, wrh_ref[...], preferred_element_type=F32)
                  + jnp.dot(xh, wrl_ref[...], preferred_element_type=F32))
        i1, i2, g1, g2 = _top2(logits)
        lane = lax.broadcasted_iota(jnp.int32, logits.shape, 1)
        r_ref[...] = jnp.where(lane == 0, i1.astype(F32),
                               jnp.where(lane == 1, i2.astype(F32),
                                         jnp.where(lane == 2, g1, jnp.where(lane == 3, g2, 0.0))))


def _outproj(ys, w, x, g, b, router=None):
    t = x.shape[0]
    tm = TM_PROJ
    row = lambda i: (i, 0)
    fixed = lambda i: (0, 0)
    in_specs = [pl.BlockSpec((tm, GROUP_W), row)] * 4 + [
        pl.BlockSpec((D_MODEL, D_MODEL), fixed), pl.BlockSpec((tm, D_MODEL), row),
        pl.BlockSpec((1, D_MODEL), fixed), pl.BlockSpec((1, D_MODEL), fixed)]
    args = list(ys) + [w, x, g, b]
    out_shape = [jax.ShapeDtypeStruct((t, D_MODEL), F32)]
    out_specs = [pl.BlockSpec((tm, D_MODEL), row)]
    if router is not None:
        in_specs += [pl.BlockSpec((D_MODEL, LANE), fixed)] * 2
        args += list(router)
        out_shape.append(jax.ShapeDtypeStruct((t, LANE), F32))
        out_specs.append(pl.BlockSpec((tm, LANE), row))
    return pl.pallas_call(
        functools.partial(_outproj_body, router is not None),
        grid=(t // tm,), in_specs=in_specs, out_specs=out_specs, out_shape=out_shape,
        compiler_params=_cparams(("parallel",)), name="outproj_route" if router is not None else "outproj",
    )(*args)


def _swiglu(xb, wg, wu):
    hg = jnp.dot(xb, wg, preferred_element_type=F32)
    hu = jnp.dot(xb, wu, preferred_element_type=F32)
    return (hg * jax.nn.sigmoid(hg) * hu).astype(BF16)


def _ffn_body(x_ref, wg_ref, wu_ref, wd_ref, g_ref, b_ref, o_ref, xb_ref, acc_ref):
    j = pl.program_id(1)

    @pl.when(j == 0)
    def _():
        xb_ref[...] = x_ref[...].astype(BF16)
        acc_ref[...] = jnp.zeros_like(acc_ref)

    h = _swiglu(xb_ref[...], wg_ref[...], wu_ref[...])
    acc_ref[...] += jnp.dot(h, wd_ref[...], preferred_element_type=F32)

    @pl.when(j == pl.num_programs(1) - 1)
    def _():
        o_ref[...] = _layer_norm(DEEPNORM_ALPHA * x_ref[...] + acc_ref[...], g_ref[...], b_ref[...])


def _ffn(x, wg, wu, wd, g, b):
    t = x.shape[0]
    tm, tf = TM_FFN, TF_FFN
    return pl.pallas_call(
        _ffn_body,
        grid=(t // tm, D_FF // tf),
        in_specs=[pl.BlockSpec((tm, D_MODEL), lambda i, j: (i, 0)),
                  pl.BlockSpec((D_MODEL, tf), lambda i, j: (0, j)),
                  pl.BlockSpec((D_MODEL, tf), lambda i, j: (0, j)),
                  pl.BlockSpec((tf, D_MODEL), lambda i, j: (j, 0)),
                  pl.BlockSpec((1, D_MODEL), lambda i, j: (0, 0)),
                  pl.BlockSpec((1, D_MODEL), lambda i, j: (0, 0))],
        out_specs=pl.BlockSpec((tm, D_MODEL), lambda i, j: (i, 0)),
        out_shape=jax.ShapeDtypeStruct((t, D_MODEL), F32),
        scratch_shapes=[pltpu.VMEM((tm, D_MODEL), BF16), pltpu.VMEM((tm, D_MODEL), F32)],
        compiler_params=_cparams(("parallel", "arbitrary")), name="ffn_dense",
    )(x, wg, wu, wd, g, b)


def _row_copy(src_hbm, row, dst_ref, slot, sem):
    return pltpu.make_async_copy(src_hbm.at[pl.ds(row, 1)], dst_ref.at[pl.ds(slot, 1)], sem)


def _gather_rows(idx_ref, src_hbm, dst_ref, sem, n):
    def start(r, c):
        _row_copy(src_hbm, idx_ref[0, 0, r], dst_ref, r, sem).start()
        return c

    def wait(r, c):
        _row_copy(src_hbm, 0, dst_ref, r, sem).wait()
        return c

    lax.fori_loop(0, n, start, 0)
    lax.fori_loop(0, n, wait, 0)


def _gather_body(idx_ref, x_hbm, o_ref, sem):
    _gather_rows(idx_ref, x_hbm, o_ref, sem, o_ref.shape[0])


def _gather(x, src3):
    nt, _, tm = src3.shape
    return pl.pallas_call(
        _gather_body,
        grid=(nt,),
        in_specs=[pl.BlockSpec((1, 1, tm), lambda i: (i, 0, 0), memory_space=pltpu.SMEM),
                  pl.BlockSpec(memory_space=pl.ANY)],
        out_specs=pl.BlockSpec((tm, D_MODEL), lambda i: (i, 0)),
        out_shape=jax.ShapeDtypeStruct((nt * tm, D_MODEL), x.dtype),
        scratch_shapes=[pltpu.SemaphoreType.DMA(())],
        compiler_params=_cparams(("arbitrary",)), name="moe_gather",
    )(src3, x)


def _moe_body(te_ref, nv_ref, x_ref, wg_ref, wu_ref, wd_ref, o_ref, xb_ref, acc_ref):
    i = pl.program_id(0)
    j = pl.program_id(1)
    last = pl.num_programs(1) - 1
    live = i < nv_ref[0]

    @pl.when(live & (j == 0))
    def _():
        xb_ref[...] = x_ref[...].astype(BF16)
        acc_ref[...] = jnp.zeros_like(acc_ref)

    @pl.when(live)
    def _():
        h = _swiglu(xb_ref[...], wg_ref[0], wu_ref[0])
        acc_ref[...] += jnp.dot(h, wd_ref[0], preferred_element_type=F32)

    @pl.when(live & (j == last))
    def _():
        o_ref[...] = acc_ref[...]

    @pl.when(jnp.logical_not(live) & (j == last))
    def _():
        o_ref[...] = jnp.zeros_like(o_ref)


def _moe(xs, tile_expert, n_valid, wg, wu, wd):
    p = xs.shape[0]
    tm, tf = TM_MOE, TF_MOE
    nf = D_FF_EXPERT // tf

    def jj(i, j, nv):
        return jnp.where(i < nv[0], j, nf - 1)

    grid_spec = pltpu.PrefetchScalarGridSpec(
        num_scalar_prefetch=2,
        grid=(p // tm, nf),
        in_specs=[pl.BlockSpec((tm, D_MODEL), lambda i, j, te, nv: (i, 0)),
                  pl.BlockSpec((1, D_MODEL, tf), lambda i, j, te, nv: (te[i], 0, jj(i, j, nv))),
                  pl.BlockSpec((1, D_MODEL, tf), lambda i, j, te, nv: (te[i], 0, jj(i, j, nv))),
                  pl.BlockSpec((1, tf, D_MODEL), lambda i, j, te, nv: (te[i], jj(i, j, nv), 0))],
        out_specs=pl.BlockSpec((tm, D_MODEL), lambda i, j, te, nv: (i, 0)),
        scratch_shapes=[pltpu.VMEM((tm, D_MODEL), BF16), pltpu.VMEM((tm, D_MODEL), F32)],
    )
    return pl.pallas_call(
        _moe_body, grid_spec=grid_spec,
        out_shape=jax.ShapeDtypeStruct((p, D_MODEL), F32),
        compiler_params=_cparams(("arbitrary", "arbitrary")), name="moe_experts",
    )(tile_expert, n_valid, xs, wg, wu, wd)


def _combine_body(p1_ref, p2_ref, ys_hbm, x_ref, r_ref, g_ref, b_ref, o_ref, r1_ref, r2_ref, sem):
    tm = o_ref.shape[0]
    _gather_rows(p1_ref, ys_hbm, r1_ref, sem.at[0], tm)
    _gather_rows(p2_ref, ys_hbm, r2_ref, sem.at[1], tm)
    r = r_ref[...]
    f = r[:, 2:3] * r1_ref[...] + r[:, 3:4] * r2_ref[...]
    o_ref[...] = _layer_norm(DEEPNORM_ALPHA * x_ref[...] + f, g_ref[...], b_ref[...])


def _combine(pos1, pos2, ys, x, route, g, b):
    t = x.shape[0]
    tm = TM_COMB
    smem = pl.BlockSpec((1, 1, tm), lambda i: (i, 0, 0), memory_space=pltpu.SMEM)
    return pl.pallas_call(
        _combine_body,
        grid=(t // tm,),
        in_specs=[smem, smem, pl.BlockSpec(memory_space=pl.ANY),
                  pl.BlockSpec((tm, D_MODEL), lambda i: (i, 0)),
                  pl.BlockSpec((tm, LANE), lambda i: (i, 0)),
                  pl.BlockSpec((1, D_MODEL), lambda i: (0, 0)),
                  pl.BlockSpec((1, D_MODEL), lambda i: (0, 0))],
        out_specs=pl.BlockSpec((tm, D_MODEL), lambda i: (i, 0)),
        out_shape=jax.ShapeDtypeStruct((t, D_MODEL), F32),
        scratch_shapes=[pltpu.VMEM((tm, D_MODEL), F32), pltpu.VMEM((tm, D_MODEL), F32),
                        pltpu.SemaphoreType.DMA((2,))],
        compiler_params=_cparams(("arbitrary",)), name="moe_combine",
    )(pos1.reshape(t // tm, 1, tm), pos2.reshape(t // tm, 1, tm), ys, x, route, g, b)


def _dispatch_plan(route, tm):
    t = route.shape[0]
    n_slots = 2 * t
    n_tiles = n_slots // tm + N_EXPERTS
    experts = route[:, 0:2].astype(jnp.int32).reshape(n_slots)
    onehot = (experts[:, None] == jnp.arange(N_EXPERTS, dtype=jnp.int32)[None, :]).astype(jnp.int32)
    rank = jnp.sum((jnp.cumsum(onehot, axis=0) - onehot) * onehot, axis=1)
    counts = jnp.sum(onehot, axis=0)
    tiles_per = (counts + tm - 1) // tm
    tile_end = jnp.cumsum(tiles_per)
    offsets = (tile_end - tiles_per) * tm
    pos = offsets[experts] + rank
    src = jnp.zeros((n_tiles * tm,), jnp.int32).at[pos].set(jnp.arange(n_slots, dtype=jnp.int32) // 2)
    n_valid = tile_end[-1]
    tile_ids = jnp.minimum(jnp.arange(n_tiles, dtype=jnp.int32), n_valid - 1)
    tile_expert = jnp.sum((tile_ids[:, None] >= tile_end[None, :]).astype(jnp.int32), axis=1)
    pos2 = pos.reshape(t, 2)
    return (src.reshape(n_tiles, 1, tm), tile_expert.astype(jnp.int32), n_valid.reshape(1).astype(jnp.int32),
            pos2[:, 0], pos2[:, 1])


def _proj_columns():
    dup = lambda start: [start + h * HEAD_DIM + d for h in range(2) for _ in range(2) for d in range(HEAD_DIM)]
    rng = lambda start: list(range(start, start + GROUP_W))
    cols = (rng(0) + rng(256) + dup(512) + dup(640) + rng(768) + rng(1024) + rng(1280)
            + rng(1536) + dup(1792) + dup(1920))
    return np.asarray(cols, np.int32)


def _rope_tables(seq, norm_w, scale):
    rows = seq // GRID_W
    row = jnp.repeat(jnp.arange(rows, dtype=F32), GRID_W)
    col = (jnp.arange(seq) % GRID_W).astype(F32)
    axis_dim = HEAD_DIM // 2
    inv_freq = ROPE_THETA ** (-jnp.arange(0, axis_dim, 2, dtype=F32) / axis_dim)
    ang_r, ang_c = row[:, None] * inv_freq, col[:, None] * inv_freq
    cos = jnp.concatenate([jnp.cos(ang_r)] * 2 + [jnp.cos(ang_c)] * 2, axis=1)
    sin = jnp.concatenate([-jnp.sin(ang_r), jnp.sin(ang_r), -jnp.sin(ang_c), jnp.sin(ang_c)], axis=1)
    w_swapped = norm_w.reshape(2, 2, 16)[:, ::-1, :].reshape(HEAD_DIM)
    t1 = cos * (norm_w * scale)[None, :]
    t2 = sin * (w_swapped * scale)[None, :]
    return jnp.tile(t1, (1, 4)), jnp.tile(t2, (1, 4))


def _block_diag(blocks):
    n, c, _ = blocks.shape
    out = jnp.zeros((n * c, n * c), blocks.dtype)
    for i in range(n):
        out = out.at[i * c:(i + 1) * c, i * c:(i + 1) * c].set(blocks[i])
    return out


def kernel(x, ln_in_g, ln_in_b, w_in, w_pool, pool_scale, qn_w, kn_w, lam_q1, lam_k1, lam_q2, lam_k2, subln_w, sink, w_out, ln1_g, ln1_b, w_gate, w_up, w_down, w_router, e_gate, e_up, e_down, ln2_g, ln2_b):
    b, s, d = x.shape
    t = b * s
    cols = _proj_columns()
    slopes_all = [2.0 ** (-8.0 * (i + 1) / N_ALIBI) for i in range(N_ALIBI)]
    slopes_c, slopes_d = tuple(slopes_all[0::2]), tuple(slopes_all[1::2])
    bd = _block_diag(jnp.full((4, HEAD_DIM, HEAD_DIM), 1.0 / HEAD_DIM, F32)).astype(BF16)
    row2 = lambda v: v.reshape(1, -1)

    xc = x.reshape(t, d)
    for l in range(DEPTH):
        w_l = w_in[l][:, cols].astype(BF16)
        t1q, t2q = _rope_tables(s, qn_w[l], HEAD_DIM ** -0.5 * LOG2E)
        t1k, t2k = _rope_tables(s, kn_w[l], 1.0)
        if l == 0:
            xc, proj = _inproj(xc, row2(ln_in_g), row2(ln_in_b), w_l, bd, t1q, t2q, t1k, t2k, True, s)
        else:
            (proj,) = _inproj(xc, None, None, w_l, bd, t1q, t2q, t1k, t2k, False, s)
        proj3 = proj.reshape(b, s, PROJ_W)

        y_a = _pool(proj3, _block_diag(w_pool[l]).astype(BF16), row2(pool_scale[l]))
        y_b = _attn_b(proj3)
        lam_init = 0.8 - 0.6 * math.exp(-0.3 * l)
        lam_vecs = jnp.stack([lam_q1[l], lam_k1[l], lam_q2[l], lam_k2[l]])
        subln = row2(jnp.tile(subln_w[l], 2) * (1.0 - lam_init))
        y_c = _attn_c(proj3, lam_vecs, subln, lam_init, slopes_c)
        y_d = _attn_d(proj3, sink[l] * LOG2E, slopes_d)
        ys = [y.reshape(t, GROUP_W) for y in (y_a, y_b, y_c, y_d)]

        w_o = w_out[l].astype(BF16)
        i = l // 2
        if l % 2 == 0:
            (x1,) = _outproj(ys, w_o, xc, row2(ln1_g[l]), row2(ln1_b[l]))
            xc = _ffn(x1, w_gate[i].astype(BF16), w_up[i].astype(BF16), w_down[i].astype(BF16),
                      row2(ln2_g[l]), row2(ln2_b[l]))
        else:
            wr = jnp.zeros((d, LANE), F32).at[:, :N_EXPERTS].set(w_router[i])
            wr_hi = wr.astype(BF16)
            wr_lo = (wr - wr_hi.astype(F32)).astype(BF16)
            x1, route = _outproj(ys, w_o, xc, row2(ln1_g[l]), row2(ln1_b[l]), router=(wr_hi, wr_lo))
            src3, tile_expert, n_valid, pos1, pos2 = _dispatch_plan(route, TM_MOE)
            xs = _gather(x1, src3)
            y_s = _moe(xs, tile_expert, n_valid, e_gate[i].astype(BF16), e_up[i].astype(BF16),
                       e_down[i].astype(BF16))
            xc = _combine(pos1, pos2, y_s, x1, route, row2(ln2_g[l]), row2(ln2_b[l]))
    return xc.reshape(b, s, d)
```
